```python
import jax, jax.numpy as jnp
from jax import lax
import numpy as np

D_MODEL = 1024
BATCH = 16
SEQ = 4096
DEPTH = 2

N_MIXERS = 2
N_A = (DEPTH + 1) // 2
N_B = DEPTH // 2

CHUNK = 128
SGU_WIDTH = D_MODEL
SGU_GROUPS = 8
SGU_GROUP_DIM = SGU_WIDTH // SGU_GROUPS

RWKV_HEAD = 64
RWKV_HEADS = D_MODEL // RWKV_HEAD
DECAY_LORA = 64
AAA_LORA = 64
GATE_LORA = 160

D_FF = 4 * D_MODEL
N_MOD = 6

RMS_EPS = 1e-6
LN_EPS = 1e-5
GN_EPS = RWKV_HEAD * 1e-5
L2_EPS = 1e-12

kernel_name = "hybrid_sgu_rwkv7_adaln_trunk"


def rms_norm(x):
    xf = x.astype(jnp.float32)
    y = xf * lax.rsqrt(jnp.mean(xf * xf, axis=-1, keepdims=True) + RMS_EPS)
    return y.astype(x.dtype)


def layer_norm(x, g, b):
    xf = x.astype(jnp.float32)
    mu = jnp.mean(xf, axis=-1, keepdims=True)
    var = jnp.mean(jnp.square(xf - mu), axis=-1, keepdims=True)
    y = (xf - mu) * lax.rsqrt(var + LN_EPS)
    return y.astype(x.dtype) * g + b


def modulate(h, shift, scale):
    return h * (1 + scale[:, None, :]) + shift[:, None, :]


def token_shift(x):
    return jnp.pad(x[:, :-1], ((0, 0), (1, 0), (0, 0)))


def sgu_mixer(h, w_in, ln_g, ln_b, w_s, b_s, w_out):
    B, T, _ = h.shape
    uv = jax.nn.gelu(h @ w_in, approximate=False)
    u, v = jnp.split(uv, 2, axis=-1)
    v = layer_norm(v, ln_g, ln_b)
    vc = v.reshape(B, T // CHUNK, CHUNK, SGU_GROUPS, SGU_GROUP_DIM)
    mask = jnp.tril(jnp.ones((CHUNK, CHUNK), dtype=w_s.dtype))
    sv = jnp.einsum('gts,bcsgd->bctgd', w_s * mask, vc) + b_s.T[:, :, None]
    return (u * sv.reshape(B, T, SGU_WIDTH)) @ w_out


def wkv7_scan(r, w, k, v, a_, b_):
    B, T, H, N = r.shape
    seq = tuple(jnp.swapaxes(z.astype(jnp.float32), 0, 1) for z in (r, w, k, v, a_, b_))

    def step(S, inp):
        r_t, w_t, k_t, v_t, a_t, b_t = inp
        sa = jnp.einsum('bhij,bhj->bhi', S, a_t)
        S = S * w_t[:, :, None, :] + sa[..., None] * b_t[:, :, None, :] + v_t[..., None] * k_t[:, :, None, :]
        y = jnp.einsum('bhij,bhj->bhi', S, r_t)
        return S, y

    S0 = jnp.zeros((B, H, N, N), jnp.float32)
    _, ys = lax.scan(step, S0, seq)
    return jnp.swapaxes(ys, 0, 1).astype(r.dtype)


def rwkv7_mixer(h, mu, w_in, w0, w1, w2, a0, a1, a2, g1, g2, k_k, k_a, r_k, ln_g, ln_b, w_out):
    B, T, D = h.shape
    H, N = RWKV_HEADS, RWKV_HEAD
    xx = token_shift(h) - h
    xr, xw, xk, xv, xa, xg = [h + xx * mu[i] for i in range(6)]
    rkv = jnp.einsum('nbtd,dne->nbte', jnp.stack([xr, xk, xv]), w_in.reshape(D, 3, D))
    r, k, v = rkv[0], rkv[1], rkv[2]
    w_log = -jax.nn.softplus(-(w0 + jnp.tanh(xw @ w1) @ w2)) - 0.5
    decay = jnp.exp(-jnp.exp(w_log))
    a = jax.nn.sigmoid(a0 + (xa @ a1) @ a2)
    g = jax.nn.sigmoid(xg @ g1) @ g2
    kk = (k * k_k).reshape(B, T, H, N)
    kkf = kk.astype(jnp.float32)
    kk = (kkf / jnp.maximum(jnp.sqrt(jnp.sum(kkf * kkf, axis=-1, keepdims=True)), L2_EPS)).astype(h.dtype)
    k = k * (1 + (a - 1) * k_a)
    rh = r.reshape(B, T, H, N)
    kh = k.reshape(B, T, H, N)
    vh = v.reshape(B, T, H, N)
    ah = a.reshape(B, T, H, N)
    y = wkv7_scan(rh, decay.reshape(B, T, H, N), kh, vh, -kk, kk * ah)
    yf = y.astype(jnp.float32)
    m = jnp.mean(yf, axis=-1, keepdims=True)
    var = jnp.mean(jnp.square(yf - m), axis=-1, keepdims=True)
    y = ((yf - m) * lax.rsqrt(var + GN_EPS)).astype(h.dtype).reshape(B, T, D) * ln_g + ln_b
    bonus = jnp.sum(rh * kh * r_k, axis=-1, keepdims=True) * vh
    y = y + bonus.reshape(B, T, D)
    return (y * g) @ w_out


def setup_inputs(seed: int = 0) -> dict:
    key = jax.random.key(seed)
    ks = iter(jax.random.split(key, 48))
    D = D_MODEL

    def nrm(shape, scale):
        return jax.random.normal(next(ks), shape, jnp.float32) * scale

    def unif(shape, lo, hi):
        return jax.random.uniform(next(ks), shape, jnp.float32, lo, hi)

    return {
        "x": nrm((BATCH, SEQ, D), 1.0),
        "c": nrm((BATCH, D), 1.0),
        "ada_w": nrm((DEPTH, D, N_MOD * D), 0.3 * D ** -0.5),
        "ada_b": nrm((DEPTH, N_MOD * D), 0.02),
        "mlp_w1": nrm((DEPTH, D, D_FF), D ** -0.5),
        "mlp_w2": nrm((DEPTH, D_FF, D), D_FF ** -0.5),
        "a_w_in": nrm((N_A, D, 2 * SGU_WIDTH), D ** -0.5),
        "a_ln_g": 1.0 + nrm((N_A, SGU_WIDTH), 0.02),
        "a_ln_b": nrm((N_A, SGU_WIDTH), 0.02),
        "a_w_s": nrm((N_A, SGU_GROUPS, CHUNK, CHUNK), CHUNK ** -0.5),
        "a_b_s": 1.0 + nrm((N_A, SGU_GROUPS, CHUNK), 0.02),
        "a_w_out": nrm((N_A, SGU_WIDTH, D), SGU_WIDTH ** -0.5),
        "b_mu": unif((N_B, 6, D), 0.0, 1.0),
        "b_w_in": nrm((N_B, D, 3 * D), D ** -0.5),
        "b_w0": unif((N_B, D), -7.0, -2.0),
        "b_w1": nrm((N_B, D, DECAY_LORA), D ** -0.5),
        "b_w2": nrm((N_B, DECAY_LORA, D), 0.1 * DECAY_LORA ** -0.5),
        "b_a0": nrm((N_B, D), 0.5),
        "b_a1": nrm((N_B, D, AAA_LORA), D ** -0.5),
        "b_a2": nrm((N_B, AAA_LORA, D), 0.5 * AAA_LORA ** -0.5),
        "b_g1": nrm((N_B, D, GATE_LORA), D ** -0.5),
        "b_g2": nrm((N_B, GATE_LORA, D), GATE_LORA ** -0.5),
        "b_k_k": 0.85 + nrm((N_B, D), 0.02),
        "b_k_a": 1.0 + nrm((N_B, D), 0.02),
        "b_r_k": -0.04 + nrm((N_B, RWKV_HEADS, RWKV_HEAD), 0.02),
        "b_ln_g": 1.0 + nrm((N_B, D), 0.02),
        "b_ln_b": nrm((N_B, D), 0.02),
        "b_w_out": nrm((N_B, D, D), D ** -0.5),
        "final_g": 1.0 + nrm((D,), 0.02),
    }


def reference(x, c, ada_w, ada_b, mlp_w1, mlp_w2,
              a_w_in, a_ln_g, a_ln_b, a_w_s, a_b_s, a_w_out,
              b_mu, b_w_in, b_w0, b_w1, b_w2, b_a0, b_a1, b_a2, b_g1, b_g2,
              b_k_k, b_k_a, b_r_k, b_ln_g, b_ln_b, b_w_out, final_g):
    cond = jax.nn.silu(c)
    for i in range(DEPTH):
        mod = cond @ ada_w[i] + ada_b[i]
        shift1, scale1, gate1, shift2, scale2, gate2 = jnp.split(mod, N_MOD, axis=-1)
        h = modulate(rms_norm(x), shift1, scale1)
        j = i // N_MIXERS
        if i % N_MIXERS == 0:
            mix = sgu_mixer(h, a_w_in[j], a_ln_g[j], a_ln_b[j], a_w_s[j], a_b_s[j], a_w_out[j])
        else:
            mix = rwkv7_mixer(h, b_mu[j], b_w_in[j], b_w0[j], b_w1[j], b_w2[j],
                              b_a0[j], b_a1[j], b_a2[j], b_g1[j], b_g2[j],
                              b_k_k[j], b_k_a[j], b_r_k[j], b_ln_g[j], b_ln_b[j], b_w_out[j])
        x = x + gate1[:, None, :] * mix
        h = modulate(rms_norm(x), shift2, scale2)
        ff = jnp.square(jax.nn.relu(h @ mlp_w1[i])) @ mlp_w2[i]
        x = x + gate2[:, None, :] * ff
    return rms_norm(x) * final_g
```

```python
import functools
import math

import jax
import jax.numpy as jnp
from jax import lax
from jax.experimental import pallas as pl
from jax.experimental.pallas import tpu as pltpu

RMS_EPS = 1e-6
LN_EPS = 1e-5
L2_EPS = 1e-12
N_MOD = 6

SGU_CHUNK = 128
SGU_GROUPS = 8

HEAD = 64
GN_EPS = HEAD * 1e-5
WKV_CHUNK = 64
GROUP_LANES = 256
GROUP_HEADS = GROUP_LANES // HEAD

VMEM_LIMIT_BYTES = 56 * 1024 * 1024

F32 = jnp.float32
BF16 = jnp.bfloat16


def _bf(x):
    return x.astype(BF16)


def _dot(a, b):
    return jnp.dot(a, b, preferred_element_type=F32)


def _dot_nt(a, b):
    return lax.dot_general(a, b, (((1,), (1,)), ((), ())), preferred_element_type=F32)


def _dot_tn(a, b):
    return lax.dot_general(a, b, (((0,), (0,)), ((), ())), preferred_element_type=F32)


def _split_bf16(x):
    hi = _bf(x)
    lo = _bf(x - hi.astype(F32))
    return hi, lo


def _sigmoid(x):
    return 1.0 / (1.0 + jnp.exp(-x))


def _rms_modulate(x, shift, scale):
    ms = jnp.mean(x * x, axis=-1, keepdims=True)
    return x * lax.rsqrt(ms + RMS_EPS) * (1.0 + scale) + shift


def _head_sum(x, ones_bd):
    hi, lo = _split_bf16(x)
    parts = []
    for g in range(x.shape[1] // GROUP_LANES):
        sl = slice(g * GROUP_LANES, (g + 1) * GROUP_LANES)
        parts.append(_dot(hi[:, sl], ones_bd) + _dot(lo[:, sl], ones_bd))
    return jnp.concatenate(parts, axis=1)


def _head_block_mask(rows, cols):
    r = lax.broadcasted_iota(jnp.int32, (rows, cols), 0) // HEAD
    c = lax.broadcasted_iota(jnp.int32, (rows, cols), 1) // HEAD
    return r == c


def _compiler_params(semantics):
    return pltpu.CompilerParams(dimension_semantics=semantics, vmem_limit_bytes=VMEM_LIMIT_BYTES)


def _const_spec(shape):
    zeros = (0,) * len(shape)
    return pl.BlockSpec(shape, lambda *_: zeros)


def _ada_kernel(c_ref, w_ref, b_ref, o_ref):
    c = c_ref[...]
    cond = c * _sigmoid(c)
    c_hi, c_lo = _split_bf16(cond)
    w_hi, w_lo = _split_bf16(w_ref[0])
    o_ref[0] = _dot(c_hi, w_hi) + _dot(c_lo, w_hi) + _dot(c_hi, w_lo) + b_ref[0]


def _ada_mod(c, ada_w, ada_b):
    depth, d, nd = ada_w.shape
    b = c.shape[0]
    bn = nd // 4
    return pl.pallas_call(
        _ada_kernel,
        grid=(depth, nd // bn),
        in_specs=[
            pl.BlockSpec((b, d), lambda i, j: (0, 0)),
            pl.BlockSpec((1, d, bn), lambda i, j: (i, 0, j)),
            pl.BlockSpec((1, 1, bn), lambda i, j: (i, 0, j)),
        ],
        out_specs=pl.BlockSpec((1, b, bn), lambda i, j: (i, 0, j)),
        out_shape=jax.ShapeDtypeStruct((depth, b, nd), F32),
        compiler_params=_compiler_params(("parallel", "parallel")),
    )(c, ada_w, ada_b.reshape(depth, 1, nd))


def _sgu_kernel(x_ref, mod_ref, w_in_ref, ln_g_ref, ln_b_ref, w_s_ref, b_s_ref, w_out_ref, o_ref):
    x = x_ref[0]
    mod = mod_ref[0]
    tm, d = x.shape
    width = w_out_ref.shape[0]
    gd = width // SGU_GROUPS
    h = _rms_modulate(x, mod[0:1], mod[1:2])
    uv = _dot(_bf(h), w_in_ref[...])
    uv = 0.5 * uv * (1.0 + lax.erf(uv * (1.0 / math.sqrt(2.0))))
    u = uv[:, :width]
    v = uv[:, width:]
    mu = jnp.mean(v, axis=-1, keepdims=True)
    vc = v - mu
    var = jnp.mean(vc * vc, axis=-1, keepdims=True)
    v = vc * lax.rsqrt(var + LN_EPS) * ln_g_ref[...] + ln_b_ref[...]
    v = _bf(v)
    row = lax.broadcasted_iota(jnp.int32, (SGU_CHUNK, SGU_CHUNK), 0)
    col = lax.broadcasted_iota(jnp.int32, (SGU_CHUNK, SGU_CHUNK), 1)
    causal = col <= row
    n_chunks = tm // SGU_CHUNK
    cols = []
    for g in range(SGU_GROUPS):
        w_g = _bf(jnp.where(causal, w_s_ref[g], 0.0))
        v_g = jnp.concatenate(
            [v[c * SGU_CHUNK:(c + 1) * SGU_CHUNK, g * gd:(g + 1) * gd] for c in range(n_chunks)], axis=1)
        sv_g = _dot(w_g, v_g) + b_s_ref[:, g:g + 1]
        cols.append(jnp.concatenate(
            [sv_g[:, c * gd:(c + 1) * gd] for c in range(n_chunks)], axis=0))
    sv = jnp.concatenate(cols, axis=1)
    mix = _dot(_bf(u * sv), w_out_ref[...])
    o_ref[0] = x + mod[2:3] * mix


def _sgu_layer(x, mod, w_in, ln_g, ln_b, w_s, b_s, w_out, tm):
    b, t, d = x.shape
    width = w_out.shape[0]
    tok = pl.BlockSpec((1, tm, d), lambda i, j: (i, j, 0))
    return pl.pallas_call(
        _sgu_kernel,
        grid=(b, t // tm),
        in_specs=[
            tok,
            pl.BlockSpec((1, N_MOD, d), lambda i, j: (i, 0, 0)),
            _const_spec((d, 2 * width)),
            _const_spec((1, width)),
            _const_spec((1, width)),
            _const_spec((SGU_GROUPS, SGU_CHUNK, SGU_CHUNK)),
            _const_spec((SGU_CHUNK, SGU_GROUPS)),
            _const_spec((width, d)),
        ],
        out_specs=tok,
        out_shape=jax.ShapeDtypeStruct(x.shape, F32),
        compiler_params=_compiler_params(("parallel", "parallel")),
    )(x, mod, _bf(w_in), ln_g.reshape(1, width), ln_b.reshape(1, width), w_s, b_s.T, _bf(w_out))


def _mlp_kernel(x_ref, mod_ref, w1_ref, w2_ref, fg_ref, o_ref, *, ff_block, final_norm):
    x = x_ref[0]
    mod = mod_ref[0]
    h = _bf(_rms_modulate(x, mod[3:4], mod[4:5]))
    d_ff = w1_ref.shape[1]
    acc = jnp.zeros(x.shape, F32)
    for c in range(d_ff // ff_block):
        sl = slice(c * ff_block, (c + 1) * ff_block)
        hid = jnp.maximum(_dot(h, w1_ref[:, sl]), 0.0)
        acc = acc + _dot(_bf(hid * hid), w2_ref[sl, :])
    y = x + mod[5:6] * acc
    if final_norm:
        ms = jnp.mean(y * y, axis=-1, keepdims=True)
        y = y * lax.rsqrt(ms + RMS_EPS) * fg_ref[...]
    o_ref[0] = y


def _mlp_layer(x, mod, w1, w2, final_g, tm, final_norm):
    b, t, d = x.shape
    d_ff = w1.shape[1]
    tok = pl.BlockSpec((1, tm, d), lambda i, j: (i, j, 0))
    kern = functools.partial(_mlp_kernel, ff_block=d_ff // 4, final_norm=final_norm)
    return pl.pallas_call(
        kern,
        grid=(b, t // tm),
        in_specs=[
            tok,
            pl.BlockSpec((1, N_MOD, d), lambda i, j: (i, 0, 0)),
            _const_spec((d, d_ff)),
            _const_spec((d_ff, d)),
            _const_spec((1, d)),
        ],
        out_specs=tok,
        out_shape=jax.ShapeDtypeStruct(x.shape, F32),
        compiler_params=_compiler_params(("parallel", "parallel")),
    )(x, mod, _bf(w1), _bf(w2), final_g.reshape(1, d))


def _rwkv_proj_kernel(x_ref, mod_ref, mu_ref, w_in_ref, w0_ref, w1_ref, w2_ref, a0_ref, a1_ref, a2_ref,
                      g1_ref, g2_ref, kk_ref, ka_ref,
                      r_out, lw_out, k_out, v_out, kk_out, b_out, g_out, carry_ref):
    @pl.when(pl.program_id(1) == 0)
    def _():
        carry_ref[...] = jnp.zeros_like(carry_ref)

    x = x_ref[0]
    mod = mod_ref[0]
    tm, d = x.shape
    h = _rms_modulate(x, mod[0:1], mod[1:2])
    rolled = pltpu.roll(h, shift=1, axis=0)
    first = lax.broadcasted_iota(jnp.int32, (tm, d), 0) == 0
    prev = jnp.where(first, carry_ref[0:1, :], rolled)
    carry_ref[0:1, :] = h[tm - 1:tm, :]
    xx = prev - h
    mu = mu_ref[...]
    xr, xw, xk, xv, xa, xg = [_bf(h + xx * mu[i:i + 1]) for i in range(6)]

    r = _dot(xr, w_in_ref[:, 0:d])
    k = _dot(xk, w_in_ref[:, d:2 * d])
    v = _dot(xv, w_in_ref[:, 2 * d:3 * d])
    wl = _dot(_bf(jnp.tanh(_dot(xw, w1_ref[...]))), w2_ref[...])
    z = -(w0_ref[...] + wl)
    softplus = jnp.maximum(z, 0.0) + jnp.log(1.0 + jnp.exp(-jnp.abs(z)))
    w_log = -softplus - 0.5
    lw = -jnp.exp(w_log)
    a = _sigmoid(a0_ref[...] + _dot(_bf(_dot(xa, a1_ref[...])), a2_ref[...]))
    g = _dot(_bf(_sigmoid(_dot(xg, g1_ref[...]))), g2_ref[...])

    kk = k * kk_ref[...]
    ones_bd = jnp.where(_head_block_mask(GROUP_LANES, GROUP_LANES), 1.0, 0.0).astype(BF16)
    norm = jnp.sqrt(_head_sum(kk * kk, ones_bd))
    kk = kk / jnp.maximum(norm, L2_EPS)
    k = k * (1.0 + (a - 1.0) * ka_ref[...])

    r_out[0] = _bf(r)
    lw_out[0] = lw
    k_out[0] = _bf(k)
    v_out[0] = _bf(v)
    kk_out[0] = _bf(kk)
    b_out[0] = _bf(kk * a)
    g_out[0] = _bf(g)


def _rwkv_proj(x, mod, mu, w_in, w0, w1, w2, a0, a1, a2, g1, g2, k_k, k_a, tm):
    b, t, d = x.shape
    tok = pl.BlockSpec((1, tm, d), lambda i, j: (i, j, 0))
    row = lambda p: p.reshape(1, d)
    bf_out = jax.ShapeDtypeStruct(x.shape, BF16)
    f32_out = jax.ShapeDtypeStruct(x.shape, F32)
    return pl.pallas_call(
        _rwkv_proj_kernel,
        grid=(b, t // tm),
        in_specs=[
            tok,
            pl.BlockSpec((1, N_MOD, d), lambda i, j: (i, 0, 0)),
            _const_spec(mu.shape),
            _const_spec(w_in.shape),
            _const_spec((1, d)), _const_spec(w1.shape), _const_spec(w2.shape),
            _const_spec((1, d)), _const_spec(a1.shape), _const_spec(a2.shape),
            _const_spec(g1.shape), _const_spec(g2.shape),
            _const_spec((1, d)), _const_spec((1, d)),
        ],
        out_specs=[tok] * 7,
        out_shape=[bf_out, f32_out, bf_out, bf_out, bf_out, bf_out, bf_out],
        scratch_shapes=[pltpu.VMEM((8, d), F32)],
        compiler_params=_compiler_params(("parallel", "arbitrary")),
    )(x, mod, mu, _bf(w_in), row(w0), _bf(w1), _bf(w2), row(a0), _bf(a1), _bf(a2), _bf(g1), _bf(g2),
      row(k_k), row(k_a))


def _block_diag(x, mask):
    tiled = jnp.concatenate([x] * GROUP_HEADS, axis=0)
    return jnp.where(mask, tiled, jnp.zeros_like(tiled))


def _wkv_kernel(r_ref, lw_ref, k_ref, v_ref, kk_ref, b_ref, y_ref, s_ref):
    @pl.when(pl.program_id(1) == 0)
    def _():
        s_ref[...] = jnp.zeros_like(s_ref)

    L = WKV_CHUNK
    tb, d = lw_ref.shape[1], lw_ref.shape[2]
    n_groups = d // GROUP_LANES

    t_row = lax.broadcasted_iota(jnp.int32, (L, GROUP_LANES), 0)
    s_lane = lax.broadcasted_iota(jnp.int32, (L, GROUP_LANES), 1) % HEAD
    strict = s_lane < t_row
    incl = s_lane <= t_row
    eye = jnp.where(s_lane == t_row, 1.0, 0.0)
    bd_mask = _head_block_mask(GROUP_HEADS * L, GROUP_LANES)
    tri = jnp.where(lax.broadcasted_iota(jnp.int32, (L, L), 1) <= lax.broadcasted_iota(jnp.int32, (L, L), 0),
                    1.0, 0.0)

    def chunk_body(c, carry):
        sl = pl.ds(pl.multiple_of(c * L, L), L)
        lw = lw_ref[0, sl, :]
        tri_b = _bf(tri)
        p0 = _bf(lw)
        r0 = lw - p0.astype(F32)
        p1 = _bf(r0)
        p2 = _bf(r0 - p1.astype(F32))
        cs = _dot(tri_b, p0) + _dot(tri_b, p1) + _dot(tri_b, p2)
        cs_last = cs[L - 1:L, :]
        e_in = jnp.exp(cs)
        e_ex = jnp.exp(cs - lw)
        e_inv = jnp.exp(-cs)
        e_hat = jnp.exp(cs_last - cs)
        w_last = jnp.exp(cs_last)
        kk = kk_ref[0, sl, :].astype(F32)
        bb = b_ref[0, sl, :].astype(F32)
        kx = k_ref[0, sl, :].astype(F32)
        a_t = _bf(-kk * e_ex)
        r_t = _bf(r_ref[0, sl, :].astype(F32) * e_in)
        b_t = _bf(bb * e_inv)
        k_t = _bf(kx * e_inv)
        b_h = _bf(bb * e_hat)
        k_h = _bf(kx * e_hat)
        vv = v_ref[0, sl, :]

        for g in range(n_groups):
            gs = slice(g * GROUP_LANES, (g + 1) * GROUP_LANES)
            lhs = jnp.concatenate([a_t[:, gs], r_t[:, gs]], axis=0)
            rhs = jnp.concatenate([_block_diag(b_t[:, gs], bd_mask),
                                   _block_diag(k_t[:, gs], bd_mask)], axis=0)
            aa = _dot_nt(lhs, rhs)
            a_ab = jnp.where(strict, aa[:L, :GROUP_LANES], 0.0)
            a_ak = jnp.where(strict, aa[:L, GROUP_LANES:], 0.0)
            a_rb = jnp.where(incl, aa[L:, :GROUP_LANES], 0.0)
            a_rk = jnp.where(incl, aa[L:, GROUP_LANES:], 0.0)

            inv = eye + a_ab
            pw = _dot(_bf(a_ab), _block_diag(_bf(a_ab), bd_mask))
            n_sq = int(math.log2(L)) - 1
            for it in range(n_sq):
                pw_bd = _block_diag(_bf(pw), bd_mask)
                if it + 1 < n_sq:
                    res = _dot(_bf(jnp.concatenate([inv, pw], axis=0)), pw_bd)
                    inv = inv + res[:L]
                    pw = res[L:]
                else:
                    inv = inv + _dot(_bf(inv), pw_bd)

            v_bd = _block_diag(vv[:, gs], bd_mask)
            av = _dot(_bf(jnp.concatenate([a_ak, a_rk], axis=0)), v_bd)
            s = s_ref[g]
            ss = _dot_nt(lhs, _bf(s))
            xc = ss[:L] + av[:L]
            u = _dot(_bf(inv), _block_diag(_bf(xc), bd_mask))
            y = ss[L:] + _dot(_bf(a_rb), _block_diag(_bf(u), bd_mask)) + av[L:]
            upd = _dot_tn(jnp.concatenate([_bf(u), vv[:, gs]], axis=0),
                          jnp.concatenate([b_h[:, gs], k_h[:, gs]], axis=0))
            s_ref[g] = s * w_last[:, gs] + jnp.where(bd_mask, upd, 0.0)
            y_ref[0, sl, gs] = y
        return carry

    lax.fori_loop(0, tb // L, chunk_body, 0)


def _wkv(r, lw, k, v, kk, bq, tb):
    b, t, d = lw.shape
    tok = pl.BlockSpec((1, tb, d), lambda i, j: (i, j, 0))
    return pl.pallas_call(
        _wkv_kernel,
        grid=(b, t // tb),
        in_specs=[tok] * 6,
        out_specs=tok,
        out_shape=jax.ShapeDtypeStruct(lw.shape, F32),
        scratch_shapes=[pltpu.VMEM((d // GROUP_LANES, GROUP_LANES, GROUP_LANES), F32)],
        compiler_params=_compiler_params(("parallel", "arbitrary")),
    )(r, lw, k, v, kk, bq)


def _rwkv_out_kernel(x_ref, mod_ref, y_ref, r_ref, k_ref, v_ref, g_ref, rk_ref, ln_g_ref, ln_b_ref,
                     w_out_ref, o_ref):
    x = x_ref[0]
    mod = mod_ref[0]
    y = y_ref[0]
    ones_bd = jnp.where(_head_block_mask(GROUP_LANES, GROUP_LANES), 1.0, 0.0).astype(BF16)
    m = _head_sum(y, ones_bd) * (1.0 / HEAD)
    yc = y - m
    var = _head_sum(yc * yc, ones_bd) * (1.0 / HEAD)
    yn = yc * lax.rsqrt(var + GN_EPS) * ln_g_ref[...] + ln_b_ref[...]
    r = r_ref[0].astype(F32)
    k = k_ref[0].astype(F32)
    v = v_ref[0].astype(F32)
    bonus = _head_sum(r * k * rk_ref[...], ones_bd) * v
    out = (yn + bonus) * g_ref[0].astype(F32)
    o_ref[0] = x + mod[2:3] * _dot(_bf(out), w_out_ref[...])


def _rwkv_out(x, mod, y, r, k, v, g, r_k, ln_g, ln_b, w_out, tm):
    b, t, d = x.shape
    tok = pl.BlockSpec((1, tm, d), lambda i, j: (i, j, 0))
    row = lambda p: p.reshape(1, d)
    return pl.pallas_call(
        _rwkv_out_kernel,
        grid=(b, t // tm),
        in_specs=[
            tok,
            pl.BlockSpec((1, N_MOD, d), lambda i, j: (i, 0, 0)),
            tok, tok, tok, tok, tok,
            _const_spec((1, d)), _const_spec((1, d)), _const_spec((1, d)),
            _const_spec((d, d)),
        ],
        out_specs=tok,
        out_shape=jax.ShapeDtypeStruct(x.shape, F32),
        compiler_params=_compiler_params(("parallel", "parallel")),
    )(x, mod, y, r, k, v, g, row(r_k), row(ln_g), row(ln_b), _bf(w_out))


def kernel(x, c, ada_w, ada_b, mlp_w1, mlp_w2, a_w_in, a_ln_g, a_ln_b, a_w_s, a_b_s, a_w_out, b_mu, b_w_in,
           b_w0, b_w1, b_w2, b_a0, b_a1, b_a2, b_g1, b_g2, b_k_k, b_k_a, b_r_k, b_ln_g, b_ln_b, b_w_out,
           final_g):
    depth = ada_w.shape[0]
    bsz, t, d = x.shape
    tm = min(512, t)
    mod = _ada_mod(c, ada_w, ada_b).reshape(depth, bsz, N_MOD, d)
    for i in range(depth):
        j = i // 2
        if i % 2 == 0:
            x = _sgu_layer(x, mod[i], a_w_in[j], a_ln_g[j], a_ln_b[j], a_w_s[j], a_b_s[j], a_w_out[j], tm)
        else:
            r, lw, k, v, kk, bq, g = _rwkv_proj(
                x, mod[i], b_mu[j], b_w_in[j], b_w0[j], b_w1[j], b_w2[j], b_a0[j], b_a1[j], b_a2[j],
                b_g1[j], b_g2[j], b_k_k[j], b_k_a[j], tm)
            y = _wkv(r, lw, k, v, kk, bq, tm)
            x = _rwkv_out(x, mod[i], y, r, k, v, g, b_r_k[j], b_ln_g[j], b_ln_b[j], b_w_out[j], tm)
        x = _mlp_layer(x, mod[i], mlp_w1[i], mlp_w2[i], final_g, tm, final_norm=(i == depth - 1))
    return x
```

```python
import functools
import math

import jax
import jax.numpy as jnp
from jax import lax
from jax.experimental import pallas as pl
from jax.experimental.pallas import tpu as pltpu

RMS_EPS = 1e-6
LN_EPS = 1e-5
L2_EPS = 1e-12
N_MOD = 6

SGU_CHUNK = 128
SGU_GROUPS = 8

HEAD = 64
GN_EPS = HEAD * 1e-5
WKV_CHUNK = 64
GROUP_LANES = 256
GROUP_HEADS = GROUP_LANES // HEAD

VMEM_LIMIT_BYTES = 56 * 1024 * 1024

F32 = jnp.float32
BF16 = jnp.bfloat16


def _bf(x):
    return x.astype(BF16)


def _dot(a, b):
    return jnp.dot(a, b, preferred_element_type=F32)


def _dot_nt(a, b):
    return lax.dot_general(a, b, (((1,), (1,)), ((), ())), preferred_element_type=F32)


def _dot_tn(a, b):
    return lax.dot_general(a, b, (((0,), (0,)), ((), ())), preferred_element_type=F32)


def _split_bf16(x):
    hi = _bf(x)
    lo = _bf(x - hi.astype(F32))
    return hi, lo


def _sigmoid(x):
    return 1.0 / (1.0 + jnp.exp(-x))


def _rms_modulate(x, shift, scale):
    ms = jnp.mean(x * x, axis=-1, keepdims=True)
    return x * lax.rsqrt(ms + RMS_EPS) * (1.0 + scale) + shift


def _head_sum(x, ones_bd):
    hi, lo = _split_bf16(x)
    parts = []
    for g in range(x.shape[1] // GROUP_LANES):
        sl = slice(g * GROUP_LANES, (g + 1) * GROUP_LANES)
        parts.append(_dot(hi[:, sl], ones_bd) + _dot(lo[:, sl], ones_bd))
    return jnp.concatenate(parts, axis=1)


def _head_block_mask(rows, cols):
    r = lax.broadcasted_iota(jnp.int32, (rows, cols), 0) // HEAD
    c = lax.broadcasted_iota(jnp.int32, (rows, cols), 1) // HEAD
    return r == c


def _compiler_params(semantics):
    return pltpu.CompilerParams(dimension_semantics=semantics, vmem_limit_bytes=VMEM_LIMIT_BYTES)


def _const_spec(shape):
    zeros = (0,) * len(shape)
    return pl.BlockSpec(shape, lambda *_: zeros)


def _ada_kernel(c_ref, w_ref, b_ref, o_ref):
    c = c_ref[...]
    cond = c * _sigmoid(c)
    c_hi, c_lo = _split_bf16(cond)
    w_hi, w_lo = _split_bf16(w_ref[0])
    o_ref[0] = _dot(c_hi, w_hi) + _dot(c_lo, w_hi) + _dot(c_hi, w_lo) + b_ref[0]


def _ada_mod(c, ada_w, ada_b):
    depth, d, nd = ada_w.shape
    b = c.shape[0]
    bn = nd // 4
    return pl.pallas_call(
        _ada_kernel,
        grid=(depth, nd // bn),
        in_specs=[
            pl.BlockSpec((b, d), lambda i, j: (0, 0)),
            pl.BlockSpec((1, d, bn), lambda i, j: (i, 0, j)),
            pl.BlockSpec((1, 1, bn), lambda i, j: (i, 0, j)),
        ],
        out_specs=pl.BlockSpec((1, b, bn), lambda i, j: (i, 0, j)),
        out_shape=jax.ShapeDtypeStruct((depth, b, nd), F32),
        compiler_params=_compiler_params(("parallel", "parallel")),
    )(c, ada_w, ada_b.reshape(depth, 1, nd))


def _sgu_kernel(x_ref, mod_ref, w_in_ref, ln_g_ref, ln_b_ref, w_s_ref, b_s_ref, w_out_ref, o_ref):
    x = x_ref[0]
    mod = mod_ref[0]
    tm, d = x.shape
    width = w_out_ref.shape[0]
    gd = width // SGU_GROUPS
    h = _rms_modulate(x, mod[0:1], mod[1:2])
    uv = _dot(_bf(h), w_in_ref[...])
    uv = 0.5 * uv * (1.0 + lax.erf(uv * (1.0 / math.sqrt(2.0))))
    u = uv[:, :width]
    v = uv[:, width:]
    mu = jnp.mean(v, axis=-1, keepdims=True)
    vc = v - mu
    var = jnp.mean(vc * vc, axis=-1, keepdims=True)
    v = vc * lax.rsqrt(var + LN_EPS) * ln_g_ref[...] + ln_b_ref[...]
    v = _bf(v)
    row = lax.broadcasted_iota(jnp.int32, (SGU_CHUNK, SGU_CHUNK), 0)
    col = lax.broadcasted_iota(jnp.int32, (SGU_CHUNK, SGU_CHUNK), 1)
    causal = col <= row
    n_chunks = tm // SGU_CHUNK
    cols = []
    for g in range(SGU_GROUPS):
        w_g = _bf(jnp.where(causal, w_s_ref[g], 0.0))
        v_g = jnp.concatenate(
            [v[c * SGU_CHUNK:(c + 1) * SGU_CHUNK, g * gd:(g + 1) * gd] for c in range(n_chunks)], axis=1)
        sv_g = _dot(w_g, v_g) + b_s_ref[:, g:g + 1]
        cols.append(jnp.concatenate(
            [sv_g[:, c * gd:(c + 1) * gd] for c in range(n_chunks)], axis=0))
    sv = jnp.concatenate(cols, axis=1)
    mix = _dot(_bf(u * sv), w_out_ref[...])
    o_ref[0] = x + mod[2:3] * mix


def _sgu_layer(x, mod, w_in, ln_g, ln_b, w_s, b_s, w_out, tm):
    b, t, d = x.shape
    width = w_out.shape[0]
    tok = pl.BlockSpec((1, tm, d), lambda i, j: (i, j, 0))
    return pl.pallas_call(
        _sgu_kernel,
        grid=(b, t // tm),
        in_specs=[
            tok,
            pl.BlockSpec((1, N_MOD, d), lambda i, j: (i, 0, 0)),
            _const_spec((d, 2 * width)),
            _const_spec((1, width)),
            _const_spec((1, width)),
            _const_spec((SGU_GROUPS, SGU_CHUNK, SGU_CHUNK)),
            _const_spec((SGU_CHUNK, SGU_GROUPS)),
            _const_spec((width, d)),
        ],
        out_specs=tok,
        out_shape=jax.ShapeDtypeStruct(x.shape, F32),
        compiler_params=_compiler_params(("parallel", "parallel")),
    )(x, mod, _bf(w_in), ln_g.reshape(1, width), ln_b.reshape(1, width), w_s, b_s.T, _bf(w_out))


def _mlp_kernel(x_ref, mod_ref, w1_ref, w2_ref, fg_ref, o_ref, *, ff_block, final_norm):
    x = x_ref[0]
    mod = mod_ref[0]
    h = _bf(_rms_modulate(x, mod[3:4], mod[4:5]))
    d_ff = w1_ref.shape[1]
    acc = jnp.zeros(x.shape, F32)
    for c in range(d_ff // ff_block):
        sl = slice(c * ff_block, (c + 1) * ff_block)
        hid = jnp.maximum(_dot(h, w1_ref[:, sl]), 0.0)
        acc = acc + _dot(_bf(hid * hid), w2_ref[sl, :])
    y = x + mod[5:6] * acc
    if final_norm:
        ms = jnp.mean(y * y, axis=-1, keepdims=True)
        y = y * lax.rsqrt(ms + RMS_EPS) * fg_ref[...]
    o_ref[0] = y


def _mlp_layer(x, mod, w1, w2, final_g, tm, final_norm):
    b, t, d = x.shape
    d_ff = w1.shape[1]
    tok = pl.BlockSpec((1, tm, d), lambda i, j: (i, j, 0))
    kern = functools.partial(_mlp_kernel, ff_block=d_ff // 4, final_norm=final_norm)
    return pl.pallas_call(
        kern,
        grid=(b, t // tm),
        in_specs=[
            tok,
            pl.BlockSpec((1, N_MOD, d), lambda i, j: (i, 0, 0)),
            _const_spec((d, d_ff)),
            _const_spec((d_ff, d)),
            _const_spec((1, d)),
        ],
        out_specs=tok,
        out_shape=jax.ShapeDtypeStruct(x.shape, F32),
        compiler_params=_compiler_params(("parallel", "parallel")),
    )(x, mod, _bf(w1), _bf(w2), final_g.reshape(1, d))


def _rwkv_proj_kernel(x_ref, mod_ref, mu_ref, w_in_ref, w0_ref, w1_ref, w2_ref, a0_ref, a1_ref, a2_ref,
                      g1_ref, g2_ref, kk_ref, ka_ref,
                      r_out, lw_out, k_out, v_out, kk_out, b_out, g_out, carry_ref):
    @pl.when(pl.program_id(1) == 0)
    def _():
        carry_ref[...] = jnp.zeros_like(carry_ref)

    x = x_ref[0]
    mod = mod_ref[0]
    tm, d = x.shape
    h = _rms_modulate(x, mod[0:1], mod[1:2])
    rolled = pltpu.roll(h, shift=1, axis=0)
    first = lax.broadcasted_iota(jnp.int32, (tm, d), 0) == 0
    prev = jnp.where(first, carry_ref[0:1, :], rolled)
    carry_ref[0:1, :] = h[tm - 1:tm, :]
    xx = prev - h
    mu = mu_ref[...]
    xr, xw, xk, xv, xa, xg = [_bf(h + xx * mu[i:i + 1]) for i in range(6)]

    r = _dot(xr, w_in_ref[:, 0:d])
    k = _dot(xk, w_in_ref[:, d:2 * d])
    v = _dot(xv, w_in_ref[:, 2 * d:3 * d])
    wl = _dot(_bf(jnp.tanh(_dot(xw, w1_ref[...]))), w2_ref[...])
    z = -(w0_ref[...] + wl)
    softplus = jnp.maximum(z, 0.0) + jnp.log(1.0 + jnp.exp(-jnp.abs(z)))
    w_log = -softplus - 0.5
    lw = -jnp.exp(w_log)
    a = _sigmoid(a0_ref[...] + _dot(_bf(_dot(xa, a1_ref[...])), a2_ref[...]))
    g = _dot(_bf(_sigmoid(_dot(xg, g1_ref[...]))), g2_ref[...])

    kk = k * kk_ref[...]
    ones_bd = jnp.where(_head_block_mask(GROUP_LANES, GROUP_LANES), 1.0, 0.0).astype(BF16)
    norm = jnp.sqrt(_head_sum(kk * kk, ones_bd))
    kk = kk / jnp.maximum(norm, L2_EPS)
    k = k * (1.0 + (a - 1.0) * ka_ref[...])

    r_out[0] = _bf(r)
    lw_out[0] = lw
    k_out[0] = _bf(k)
    v_out[0] = _bf(v)
    kk_out[0] = _bf(kk)
    b_out[0] = _bf(kk * a)
    g_out[0] = _bf(g)


def _rwkv_proj(x, mod, mu, w_in, w0, w1, w2, a0, a1, a2, g1, g2, k_k, k_a, tm):
    b, t, d = x.shape
    tok = pl.BlockSpec((1, tm, d), lambda i, j: (i, j, 0))
    row = lambda p: p.reshape(1, d)
    bf_out = jax.ShapeDtypeStruct(x.shape, BF16)
    f32_out = jax.ShapeDtypeStruct(x.shape, F32)
    return pl.pallas_call(
        _rwkv_proj_kernel,
        grid=(b, t // tm),
        in_specs=[
            tok,
            pl.BlockSpec((1, N_MOD, d), lambda i, j: (i, 0, 0)),
            _const_spec(mu.shape),
            _const_spec(w_in.shape),
            _const_spec((1, d)), _const_spec(w1.shape), _const_spec(w2.shape),
            _const_spec((1, d)), _const_spec(a1.shape), _const_spec(a2.shape),
            _const_spec(g1.shape), _const_spec(g2.shape),
            _const_spec((1, d)), _const_spec((1, d)),
        ],
        out_specs=[tok] * 7,
        out_shape=[bf_out, f32_out, bf_out, bf_out, bf_out, bf_out, bf_out],
        scratch_shapes=[pltpu.VMEM((8, d), F32)],
        compiler_params=_compiler_params(("parallel", "arbitrary")),
    )(x, mod, mu, _bf(w_in), row(w0), _bf(w1), _bf(w2), row(a0), _bf(a1), _bf(a2), _bf(g1), _bf(g2),
      row(k_k), row(k_a))


def _block_diag(x, mask):
    tiled = jnp.concatenate([x] * GROUP_HEADS, axis=0)
    return jnp.where(mask, tiled, jnp.zeros_like(tiled))


def _wkv_kernel(r_ref, lw_ref, k_ref, v_ref, kk_ref, b_ref, y_ref, s_ref):
    @pl.when(pl.program_id(1) == 0)
    def _():
        s_ref[...] = jnp.zeros_like(s_ref)

    L = WKV_CHUNK
    tb, d = lw_ref.shape[1], lw_ref.shape[2]
    n_groups = d // GROUP_LANES

    t_row = lax.broadcasted_iota(jnp.int32, (L, GROUP_LANES), 0)
    s_lane = lax.broadcasted_iota(jnp.int32, (L, GROUP_LANES), 1) % HEAD
    strict = s_lane < t_row
    incl = s_lane <= t_row
    eye = jnp.where(s_lane == t_row, 1.0, 0.0)
    bd_mask = _head_block_mask(GROUP_HEADS * L, GROUP_LANES)
    tri = jnp.where(lax.broadcasted_iota(jnp.int32, (L, L), 1) <= lax.broadcasted_iota(jnp.int32, (L, L), 0),
                    1.0, 0.0)

    def chunk_body(c, carry):
        sl = pl.ds(pl.multiple_of(c * L, L), L)
        lw = lw_ref[0, sl, :]
        tri_b = _bf(tri)
        p0 = _bf(lw)
        r0 = lw - p0.astype(F32)
        p1 = _bf(r0)
        p2 = _bf(r0 - p1.astype(F32))
        cs = _dot(tri_b, p0) + _dot(tri_b, p1) + _dot(tri_b, p2)
        cs_last = cs[L - 1:L, :]
        e_in = jnp.exp(cs)
        e_ex = jnp.exp(cs - lw)
        e_inv = jnp.exp(-cs)
        e_hat = jnp.exp(cs_last - cs)
        w_last = jnp.exp(cs_last)
        kk = kk_ref[0, sl, :].astype(F32)
        bb = b_ref[0, sl, :].astype(F32)
        kx = k_ref[0, sl, :].astype(F32)
        a_t = _bf(-kk * e_ex)
        r_t = _bf(r_ref[0, sl, :].astype(F32) * e_in)
        b_t = _bf(bb * e_inv)
        k_t = _bf(kx * e_inv)
        b_h = _bf(bb * e_hat)
        k_h = _bf(kx * e_hat)
        vv = v_ref[0, sl, :]

        groups = [slice(g * GROUP_LANES, (g + 1) * GROUP_LANES) for g in range(n_groups)]
        cat = lambda a, b: jnp.concatenate([a, b], axis=0)
        bd = lambda z: _block_diag(z, bd_mask)
        lhs = [cat(a_t[:, gs], r_t[:, gs]) for gs in groups]
        aa = [_dot_nt(lhs[g], cat(bd(b_t[:, gs]), bd(k_t[:, gs]))) for g, gs in enumerate(groups)]
        a_ab = [jnp.where(strict, z[:L, :GROUP_LANES], 0.0) for z in aa]
        a_ak = [jnp.where(strict, z[:L, GROUP_LANES:], 0.0) for z in aa]
        a_rb = [_bf(jnp.where(incl, z[L:, :GROUP_LANES], 0.0)) for z in aa]
        a_rk = [jnp.where(incl, z[L:, GROUP_LANES:], 0.0) for z in aa]

        inv = [eye + z for z in a_ab]
        pw = [_dot(_bf(z), bd(_bf(z))) for z in a_ab]
        n_sq = int(math.log2(L)) - 1
        for it in range(n_sq):
            if it + 1 < n_sq:
                res = [_dot(_bf(cat(i, p)), bd(_bf(p))) for i, p in zip(inv, pw)]
                inv = [i + z[:L] for i, z in zip(inv, res)]
                pw = [z[L:] for z in res]
            else:
                inv = [i + _dot(_bf(i), bd(_bf(p))) for i, p in zip(inv, pw)]
        inv = [_bf(i) for i in inv]

        av = [_dot(_bf(cat(a_ak[g], a_rk[g])), bd(vv[:, gs])) for g, gs in enumerate(groups)]
        s_old = [s_ref[g] for g in range(n_groups)]
        ss = [_dot_nt(lhs[g], _bf(s_old[g])) for g in range(n_groups)]
        u = [_dot(inv[g], bd(_bf(ss[g][:L] + av[g][:L]))) for g in range(n_groups)]
        u = [_bf(z) for z in u]
        y = [ss[g][L:] + _dot(a_rb[g], bd(u[g])) + av[g][L:] for g in range(n_groups)]
        upd = [_dot_tn(cat(u[g], vv[:, gs]), cat(b_h[:, gs], k_h[:, gs])) for g, gs in enumerate(groups)]
        for g, gs in enumerate(groups):
            s_ref[g] = s_old[g] * w_last[:, gs] + jnp.where(bd_mask, upd[g], 0.0)
            y_ref[0, sl, gs] = y[g]
        return carry

    lax.fori_loop(0, tb // L, chunk_body, 0)


def _wkv(r, lw, k, v, kk, bq, tb):
    b, t, d = lw.shape
    tok = pl.BlockSpec((1, tb, d), lambda i, j: (i, j, 0))
    return pl.pallas_call(
        _wkv_kernel,
        grid=(b, t // tb),
        in_specs=[tok] * 6,
        out_specs=tok,
        out_shape=jax.ShapeDtypeStruct(lw.shape, F32),
        scratch_shapes=[pltpu.VMEM((d // GROUP_LANES, GROUP_LANES, GROUP_LANES), F32)],
        compiler_params=_compiler_params(("parallel", "arbitrary")),
    )(r, lw, k, v, kk, bq)


def _rwkv_out_kernel(x_ref, mod_ref, y_ref, r_ref, k_ref, v_ref, g_ref, rk_ref, ln_g_ref, ln_b_ref,
                     w_out_ref, o_ref):
    x = x_ref[0]
    mod = mod_ref[0]
    y = y_ref[0]
    ones_bd = jnp.where(_head_block_mask(GROUP_LANES, GROUP_LANES), 1.0, 0.0).astype(BF16)
    m = _head_sum(y, ones_bd) * (1.0 / HEAD)
    yc = y - m
    var = _head_sum(yc * yc, ones_bd) * (1.0 / HEAD)
    yn = yc * lax.rsqrt(var + GN_EPS) * ln_g_ref[...] + ln_b_ref[...]
    r = r_ref[0].astype(F32)
    k = k_ref[0].astype(F32)
    v = v_ref[0].astype(F32)
    bonus = _head_sum(r * k * rk_ref[...], ones_bd) * v
    out = (yn + bonus) * g_ref[0].astype(F32)
    o_ref[0] = x + mod[2:3] * _dot(_bf(out), w_out_ref[...])


def _rwkv_out(x, mod, y, r, k, v, g, r_k, ln_g, ln_b, w_out, tm):
    b, t, d = x.shape
    tok = pl.BlockSpec((1, tm, d), lambda i, j: (i, j, 0))
    row = lambda p: p.reshape(1, d)
    return pl.pallas_call(
        _rwkv_out_kernel,
        grid=(b, t // tm),
        in_specs=[
            tok,
            pl.BlockSpec((1, N_MOD, d), lambda i, j: (i, 0, 0)),
            tok, tok, tok, tok, tok,
            _const_spec((1, d)), _const_spec((1, d)), _const_spec((1, d)),
            _const_spec((d, d)),
        ],
        out_specs=tok,
        out_shape=jax.ShapeDtypeStruct(x.shape, F32),
        compiler_params=_compiler_params(("parallel", "parallel")),
    )(x, mod, y, r, k, v, g, row(r_k), row(ln_g), row(ln_b), _bf(w_out))


def kernel(x, c, ada_w, ada_b, mlp_w1, mlp_w2, a_w_in, a_ln_g, a_ln_b, a_w_s, a_b_s, a_w_out, b_mu, b_w_in,
           b_w0, b_w1, b_w2, b_a0, b_a1, b_a2, b_g1, b_g2, b_k_k, b_k_a, b_r_k, b_ln_g, b_ln_b, b_w_out,
           final_g):
    depth = ada_w.shape[0]
    bsz, t, d = x.shape
    tm = min(512, t)
    mod = _ada_mod(c, ada_w, ada_b).reshape(depth, bsz, N_MOD, d)
    for i in range(depth):
        j = i // 2
        if i % 2 == 0:
            x = _sgu_layer(x, mod[i], a_w_in[j], a_ln_g[j], a_ln_b[j], a_w_s[j], a_b_s[j], a_w_out[j], tm)
        else:
            r, lw, k, v, kk, bq, g = _rwkv_proj(
                x, mod[i], b_mu[j], b_w_in[j], b_w0[j], b_w1[j], b_w2[j], b_a0[j], b_a1[j], b_a2[j],
                b_g1[j], b_g2[j], b_k_k[j], b_k_a[j], tm)
            y = _wkv(r, lw, k, v, kk, bq, tm)
            x = _rwkv_out(x, mod[i], y, r, k, v, g, b_r_k[j], b_ln_g[j], b_ln_b[j], b_w_out[j], tm)
        x = _mlp_layer(x, mod[i], mlp_w1[i], mlp_w2[i], final_g, tm, final_norm=(i == depth - 1))
    return x
```

```python
import functools
import math

import jax
import jax.numpy as jnp
from jax import lax
from jax.experimental import pallas as pl
from jax.experimental.pallas import tpu as pltpu

RMS_EPS = 1e-6
LN_EPS = 1e-5
L2_EPS = 1e-12
N_MOD = 6

SGU_CHUNK = 128
SGU_GROUPS = 8

HEAD = 64
GN_EPS = HEAD * 1e-5
WKV_CHUNK = 64
GROUP_LANES = 256
GROUP_HEADS = GROUP_LANES // HEAD
WKV_ROWS = 4
WKV_BLOCK = 256
PROJ_SUBTILES = 4

VMEM_LIMIT_BYTES = 56 * 1024 * 1024

F32 = jnp.float32
BF16 = jnp.bfloat16


def _bf(x):
    return x.astype(BF16)


def _dot(a, b):
    return jnp.dot(a, b, preferred_element_type=F32)


def _dot_nt(a, b):
    return lax.dot_general(a, b, (((1,), (1,)), ((), ())), preferred_element_type=F32)


def _dot_tn(a, b):
    return lax.dot_general(a, b, (((0,), (0,)), ((), ())), preferred_element_type=F32)


def _split_bf16(x):
    hi = _bf(x)
    lo = _bf(x - hi.astype(F32))
    return hi, lo


def _sigmoid(x):
    return 1.0 / (1.0 + jnp.exp(-x))


def _rms_modulate(x, shift, scale):
    ms = jnp.mean(x * x, axis=-1, keepdims=True)
    return x * lax.rsqrt(ms + RMS_EPS) * (1.0 + scale) + shift


def _head_sum(x, ones_bd):
    xb = _bf(x)
    parts = []
    for g in range(x.shape[1] // GROUP_LANES):
        sl = slice(g * GROUP_LANES, (g + 1) * GROUP_LANES)
        parts.append(_dot(xb[:, sl], ones_bd))
    return jnp.concatenate(parts, axis=1)


def _head_block_mask(rows, cols):
    r = lax.broadcasted_iota(jnp.int32, (rows, cols), 0) // HEAD
    c = lax.broadcasted_iota(jnp.int32, (rows, cols), 1) // HEAD
    return r == c


def _compiler_params(semantics):
    return pltpu.CompilerParams(dimension_semantics=semantics, vmem_limit_bytes=VMEM_LIMIT_BYTES)


def _const_spec(shape):
    zeros = (0,) * len(shape)
    return pl.BlockSpec(shape, lambda *_: zeros)


def _ada_kernel(c_ref, w_ref, b_ref, o_ref):
    c = c_ref[...]
    cond = c * _sigmoid(c)
    c_hi, c_lo = _split_bf16(cond)
    w_hi, w_lo = _split_bf16(w_ref[0])
    o_ref[0] = _dot(c_hi, w_hi) + _dot(c_lo, w_hi) + _dot(c_hi, w_lo) + b_ref[0]


def _ada_mod(c, ada_w, ada_b):
    depth, d, nd = ada_w.shape
    b = c.shape[0]
    bn = nd // 4
    return pl.pallas_call(
        _ada_kernel,
        grid=(depth, nd // bn),
        in_specs=[
            pl.BlockSpec((b, d), lambda i, j: (0, 0)),
            pl.BlockSpec((1, d, bn), lambda i, j: (i, 0, j)),
            pl.BlockSpec((1, 1, bn), lambda i, j: (i, 0, j)),
        ],
        out_specs=pl.BlockSpec((1, b, bn), lambda i, j: (i, 0, j)),
        out_shape=jax.ShapeDtypeStruct((depth, b, nd), F32),
        compiler_params=_compiler_params(("parallel", "parallel")),
    )(c, ada_w, ada_b.reshape(depth, 1, nd))


def _sgu_kernel(x_ref, mod_ref, w_in_ref, ln_g_ref, ln_b_ref, w_s_ref, b_s_ref, w_out_ref, o_ref):
    mod = mod_ref[0]
    tm = x_ref.shape[1]
    width = w_out_ref.shape[0]
    gd = width // SGU_GROUPS
    row = lax.broadcasted_iota(jnp.int32, (SGU_CHUNK, SGU_CHUNK), 0)
    col = lax.broadcasted_iota(jnp.int32, (SGU_CHUNK, SGU_CHUNK), 1)
    causal = col <= row
    w_s = [_bf(jnp.where(causal, w_s_ref[g], 0.0)) for g in range(SGU_GROUPS)]
    rows = [slice(c * SGU_CHUNK, (c + 1) * SGU_CHUNK) for c in range(tm // SGU_CHUNK)]

    def project_in(rs):
        h = _rms_modulate(x_ref[0, rs, :], mod[0:1], mod[1:2])
        return _dot(_bf(h), w_in_ref[...])

    uv_next = project_in(rows[0])
    for c, rs in enumerate(rows):
        uv = uv_next
        if c + 1 < len(rows):
            uv_next = project_in(rows[c + 1])
        uv = 0.5 * uv * (1.0 + lax.erf(uv * (1.0 / math.sqrt(2.0))))
        u = uv[:, :width]
        v = uv[:, width:]
        mu = jnp.mean(v, axis=-1, keepdims=True)
        vc = v - mu
        var = jnp.mean(vc * vc, axis=-1, keepdims=True)
        v = _bf(vc * lax.rsqrt(var + LN_EPS) * ln_g_ref[...] + ln_b_ref[...])
        sv = jnp.concatenate(
            [_dot(w_s[g], v[:, g * gd:(g + 1) * gd]) + b_s_ref[:, g:g + 1] for g in range(SGU_GROUPS)], axis=1)
        mix = _dot(_bf(u * sv), w_out_ref[...])
        o_ref[0, rs, :] = x_ref[0, rs, :] + mod[2:3] * mix


def _sgu_layer(x, mod, w_in, ln_g, ln_b, w_s, b_s, w_out, tm):
    b, t, d = x.shape
    width = w_out.shape[0]
    tok = pl.BlockSpec((1, tm, d), lambda i, j: (i, j, 0))
    return pl.pallas_call(
        _sgu_kernel,
        grid=(b, t // tm),
        in_specs=[
            tok,
            pl.BlockSpec((1, N_MOD, d), lambda i, j: (i, 0, 0)),
            _const_spec((d, 2 * width)),
            _const_spec((1, width)),
            _const_spec((1, width)),
            _const_spec((SGU_GROUPS, SGU_CHUNK, SGU_CHUNK)),
            _const_spec((SGU_CHUNK, SGU_GROUPS)),
            _const_spec((width, d)),
        ],
        out_specs=tok,
        out_shape=jax.ShapeDtypeStruct(x.shape, F32),
        compiler_params=_compiler_params(("parallel", "parallel")),
    )(x, mod, _bf(w_in), ln_g.reshape(1, width), ln_b.reshape(1, width), w_s, b_s.T, _bf(w_out))


def _mlp_kernel(x_ref, mod_ref, w1_ref, w2_ref, fg_ref, o_ref, *, ff_block, final_norm):
    x = x_ref[0]
    mod = mod_ref[0]
    h = _bf(_rms_modulate(x, mod[3:4], mod[4:5]))
    d_ff = w1_ref.shape[1]
    acc = jnp.zeros(x.shape, F32)
    for c in range(d_ff // ff_block):
        sl = slice(c * ff_block, (c + 1) * ff_block)
        hid = jnp.maximum(_dot(h, w1_ref[:, sl]), 0.0)
        acc = acc + _dot(_bf(hid * hid), w2_ref[sl, :])
    y = x + mod[5:6] * acc
    if final_norm:
        ms = jnp.mean(y * y, axis=-1, keepdims=True)
        y = y * lax.rsqrt(ms + RMS_EPS) * fg_ref[...]
    o_ref[0] = y


def _mlp_layer(x, mod, w1, w2, final_g, tm, final_norm):
    b, t, d = x.shape
    d_ff = w1.shape[1]
    tok = pl.BlockSpec((1, tm, d), lambda i, j: (i, j, 0))
    kern = functools.partial(_mlp_kernel, ff_block=d_ff // 4, final_norm=final_norm)
    return pl.pallas_call(
        kern,
        grid=(b, t // tm),
        in_specs=[
            tok,
            pl.BlockSpec((1, N_MOD, d), lambda i, j: (i, 0, 0)),
            _const_spec((d, d_ff)),
            _const_spec((d_ff, d)),
            _const_spec((1, d)),
        ],
        out_specs=tok,
        out_shape=jax.ShapeDtypeStruct(x.shape, F32),
        compiler_params=_compiler_params(("parallel", "parallel")),
    )(x, mod, _bf(w1), _bf(w2), final_g.reshape(1, d))


def _rwkv_proj_kernel(x_ref, mod_ref, mu_ref, w_in_ref, w0_ref, w1_ref, w2_ref, a0_ref, a1_ref, a2_ref,
                      g1_ref, g2_ref, kk_ref, ka_ref,
                      r_out, lw_out, k_out, v_out, kk_out, b_out, g_out, carry_ref):
    @pl.when(pl.program_id(1) == 0)
    def _():
        carry_ref[...] = jnp.zeros_like(carry_ref)

    mod = mod_ref[0]
    tm, d = x_ref.shape[1], x_ref.shape[2]
    ts = tm // PROJ_SUBTILES
    rows = [slice(s * ts, (s + 1) * ts) for s in range(PROJ_SUBTILES)]
    mu = mu_ref[...]
    ones_bd = jnp.where(_head_block_mask(GROUP_LANES, GROUP_LANES), 1.0, 0.0).astype(BF16)
    first = lax.broadcasted_iota(jnp.int32, (ts, d), 0) == 0

    last = carry_ref[0:1, :]
    for rs in rows:
        h = _rms_modulate(x_ref[0, rs, :], mod[0:1], mod[1:2])
        prev = jnp.where(first, last, pltpu.roll(h, shift=1, axis=0))
        last = h[ts - 1:ts, :]
        xx = prev - h
        xr, xw, xk, xv, xa, xg = [_bf(h + xx * mu[i:i + 1]) for i in range(6)]
        w_lo = _dot(xw, w1_ref[...])
        a_lo = _dot(xa, a1_ref[...])
        g_lo = _dot(xg, g1_ref[...])
        r = _dot(xr, w_in_ref[:, 0:d])
        k = _dot(xk, w_in_ref[:, d:2 * d])
        v = _dot(xv, w_in_ref[:, 2 * d:3 * d])
        wl = _dot(_bf(jnp.tanh(w_lo)), w2_ref[...])
        al = _dot(_bf(a_lo), a2_ref[...])
        g = _dot(_bf(_sigmoid(g_lo)), g2_ref[...])
        kk = k * kk_ref[...]
        ssq = _head_sum(kk * kk, ones_bd)
        lw = -math.exp(-0.5) * _sigmoid(w0_ref[...] + wl)
        a = _sigmoid(a0_ref[...] + al)
        kk_n = kk * lax.rsqrt(jnp.maximum(ssq, L2_EPS * L2_EPS))
        r_out[0, rs, :] = _bf(r)
        lw_out[0, rs, :] = lw
        k_out[0, rs, :] = _bf(k * (1.0 + (a - 1.0) * ka_ref[...]))
        v_out[0, rs, :] = _bf(v)
        kk_out[0, rs, :] = _bf(kk_n)
        b_out[0, rs, :] = _bf(kk_n * a)
        g_out[0, rs, :] = _bf(g)
    carry_ref[0:1, :] = last


def _rwkv_proj(x, mod, mu, w_in, w0, w1, w2, a0, a1, a2, g1, g2, k_k, k_a, tm):
    b, t, d = x.shape
    tok = pl.BlockSpec((1, tm, d), lambda i, j: (i, j, 0))
    row = lambda p: p.reshape(1, d)
    bf_out = jax.ShapeDtypeStruct(x.shape, BF16)
    f32_out = jax.ShapeDtypeStruct(x.shape, F32)
    return pl.pallas_call(
        _rwkv_proj_kernel,
        grid=(b, t // tm),
        in_specs=[
            tok,
            pl.BlockSpec((1, N_MOD, d), lambda i, j: (i, 0, 0)),
            _const_spec(mu.shape),
            _const_spec(w_in.shape),
            _const_spec((1, d)), _const_spec(w1.shape), _const_spec(w2.shape),
            _const_spec((1, d)), _const_spec(a1.shape), _const_spec(a2.shape),
            _const_spec(g1.shape), _const_spec(g2.shape),
            _const_spec((1, d)), _const_spec((1, d)),
        ],
        out_specs=[tok] * 7,
        out_shape=[bf_out, f32_out, bf_out, bf_out, bf_out, bf_out, bf_out],
        scratch_shapes=[pltpu.VMEM((8, d), F32)],
        compiler_params=_compiler_params(("parallel", "arbitrary")),
    )(x, mod, mu, _bf(w_in), row(w0), _bf(w1), _bf(w2), row(a0), _bf(a1), _bf(a2), _bf(g1), _bf(g2),
      row(k_k), row(k_a))


def _block_diag(x, mask):
    tiled = jnp.concatenate([x] * GROUP_HEADS, axis=0)
    return jnp.where(mask, tiled, jnp.zeros_like(tiled))


def _wkv_kernel(r_ref, lw_ref, k_ref, v_ref, kk_ref, b_ref, y_ref, s_ref):
    @pl.when(pl.program_id(1) == 0)
    def _():
        s_ref[...] = jnp.zeros_like(s_ref)

    L = WKV_CHUNK
    nb, tb, d = lw_ref.shape
    n_groups = d // GROUP_LANES

    t_row = lax.broadcasted_iota(jnp.int32, (L, GROUP_LANES), 0)
    s_lane = lax.broadcasted_iota(jnp.int32, (L, GROUP_LANES), 1) % HEAD
    strict = s_lane < t_row
    incl = s_lane <= t_row
    eye = jnp.where(s_lane == t_row, 1.0, 0.0)
    bd_mask = _head_block_mask(GROUP_HEADS * L, GROUP_LANES)
    tri = jnp.where(lax.broadcasted_iota(jnp.int32, (L, L), 1) <= lax.broadcasted_iota(jnp.int32, (L, L), 0),
                    1.0, 0.0)

    def decayed_operands(n, sl):
        lw = lw_ref[n, sl, :]
        tri_b = _bf(tri)
        p0, p1 = _split_bf16(lw)
        cs = _dot(tri_b, p0) + _dot(tri_b, p1)
        cs_last = cs[L - 1:L, :]
        e_in = jnp.exp(cs)
        e_ex = jnp.exp(cs - lw)
        e_inv = jnp.exp(-cs)
        e_hat = jnp.exp(cs_last - cs)
        w_last = jnp.exp(cs_last)
        kk = kk_ref[n, sl, :].astype(F32)
        bb = b_ref[n, sl, :].astype(F32)
        kx = k_ref[n, sl, :].astype(F32)
        a_t = _bf(-kk * e_ex)
        r_t = _bf(r_ref[n, sl, :].astype(F32) * e_in)
        return (a_t, r_t, _bf(bb * e_inv), _bf(kx * e_inv), _bf(bb * e_hat), _bf(kx * e_hat),
                v_ref[n, sl, :], w_last)

    def chunk_body(c, carry):
        sl = pl.ds(pl.multiple_of(c * L, L), L)
        per_row = [decayed_operands(n, sl) for n in range(nb)]
        chains = [(n, g) for n in range(nb) for g in range(n_groups)]
        lanes = lambda g: slice(g * GROUP_LANES, (g + 1) * GROUP_LANES)
        pick = lambda idx: [per_row[n][idx][:, lanes(g)] for n, g in chains]
        a_t, r_t, b_t, k_t, b_h, k_h, vv, w_last = [pick(idx) for idx in range(8)]
        nc = len(chains)
        cat = lambda a, b: jnp.concatenate([a, b], axis=0)
        bd = lambda z: _block_diag(z, bd_mask)
        lhs = [cat(a, r) for a, r in zip(a_t, r_t)]
        aa = [_dot_nt(lhs[i], cat(bd(b_t[i]), bd(k_t[i]))) for i in range(nc)]
        a_ab = [jnp.where(strict, z[:L, :GROUP_LANES], 0.0) for z in aa]
        a_ak = [jnp.where(strict, z[:L, GROUP_LANES:], 0.0) for z in aa]
        a_rb = [_bf(jnp.where(incl, z[L:, :GROUP_LANES], 0.0)) for z in aa]
        a_rk = [jnp.where(incl, z[L:, GROUP_LANES:], 0.0) for z in aa]

        inv = [eye + z for z in a_ab]
        pw = [_dot(_bf(z), bd(_bf(z))) for z in a_ab]
        n_sq = int(math.log2(L)) - 1
        for it in range(n_sq):
            if it + 1 < n_sq:
                res = [_dot(_bf(cat(i, p)), bd(_bf(p))) for i, p in zip(inv, pw)]
                inv = [i + z[:L] for i, z in zip(inv, res)]
                pw = [z[L:] for z in res]
            else:
                inv = [i + _dot(_bf(i), bd(_bf(p))) for i, p in zip(inv, pw)]
        inv = [_bf(i) for i in inv]

        av = [_dot(_bf(cat(a_ak[i], a_rk[i])), bd(vv[i])) for i in range(nc)]
        s_old = [s_ref[i] for i in range(nc)]
        ss = [_dot_nt(lhs[i], _bf(s_old[i])) for i in range(nc)]
        u = [_dot(inv[i], bd(_bf(ss[i][:L] + av[i][:L]))) for i in range(nc)]
        u = [_bf(z) for z in u]
        y = [ss[i][L:] + _dot(a_rb[i], bd(u[i])) + av[i][L:] for i in range(nc)]
        upd = [_dot_tn(cat(u[i], vv[i]), cat(b_h[i], k_h[i])) for i in range(nc)]
        for i, (n, g) in enumerate(chains):
            s_ref[i] = s_old[i] * w_last[i] + jnp.where(bd_mask, upd[i], 0.0)
            y_ref[n, sl, lanes(g)] = y[i]
        return carry

    lax.fori_loop(0, tb // L, chunk_body, 0)


def _wkv(r, lw, k, v, kk, bq, tb, nb):
    b, t, d = lw.shape
    tok = pl.BlockSpec((nb, tb, d), lambda i, j: (i, j, 0))
    return pl.pallas_call(
        _wkv_kernel,
        grid=(b // nb, t // tb),
        in_specs=[tok] * 6,
        out_specs=tok,
        out_shape=jax.ShapeDtypeStruct(lw.shape, F32),
        scratch_shapes=[pltpu.VMEM((nb * (d // GROUP_LANES), GROUP_LANES, GROUP_LANES), F32)],
        compiler_params=_compiler_params(("parallel", "arbitrary")),
    )(r, lw, k, v, kk, bq)


def _rwkv_out_kernel(x_ref, mod_ref, y_ref, r_ref, k_ref, v_ref, g_ref, rk_ref, ln_g_ref, ln_b_ref,
                     w_out_ref, o_ref):
    x = x_ref[0]
    mod = mod_ref[0]
    y = y_ref[0]
    ones_bd = jnp.where(_head_block_mask(GROUP_LANES, GROUP_LANES), 1.0, 0.0).astype(BF16)
    m = _head_sum(y, ones_bd) * (1.0 / HEAD)
    yc = y - m
    var = _head_sum(yc * yc, ones_bd) * (1.0 / HEAD)
    yn = yc * lax.rsqrt(var + GN_EPS) * ln_g_ref[...] + ln_b_ref[...]
    r = r_ref[0].astype(F32)
    k = k_ref[0].astype(F32)
    v = v_ref[0].astype(F32)
    bonus = _head_sum(r * k * rk_ref[...], ones_bd) * v
    out = (yn + bonus) * g_ref[0].astype(F32)
    o_ref[0] = x + mod[2:3] * _dot(_bf(out), w_out_ref[...])


def _rwkv_out(x, mod, y, r, k, v, g, r_k, ln_g, ln_b, w_out, tm):
    b, t, d = x.shape
    tok = pl.BlockSpec((1, tm, d), lambda i, j: (i, j, 0))
    row = lambda p: p.reshape(1, d)
    return pl.pallas_call(
        _rwkv_out_kernel,
        grid=(b, t // tm),
        in_specs=[
            tok,
            pl.BlockSpec((1, N_MOD, d), lambda i, j: (i, 0, 0)),
            tok, tok, tok, tok, tok,
            _const_spec((1, d)), _const_spec((1, d)), _const_spec((1, d)),
            _const_spec((d, d)),
        ],
        out_specs=tok,
        out_shape=jax.ShapeDtypeStruct(x.shape, F32),
        compiler_params=_compiler_params(("parallel", "parallel")),
    )(x, mod, y, r, k, v, g, row(r_k), row(ln_g), row(ln_b), _bf(w_out))


def kernel(x, c, ada_w, ada_b, mlp_w1, mlp_w2, a_w_in, a_ln_g, a_ln_b, a_w_s, a_b_s, a_w_out, b_mu, b_w_in,
           b_w0, b_w1, b_w2, b_a0, b_a1, b_a2, b_g1, b_g2, b_k_k, b_k_a, b_r_k, b_ln_g, b_ln_b, b_w_out,
           final_g):
    depth = ada_w.shape[0]
    bsz, t, d = x.shape
    tm = min(512, t)
    mod = _ada_mod(c, ada_w, ada_b).reshape(depth, bsz, N_MOD, d)
    for i in range(depth):
        j = i // 2
        if i % 2 == 0:
            x = _sgu_layer(x, mod[i], a_w_in[j], a_ln_g[j], a_ln_b[j], a_w_s[j], a_b_s[j], a_w_out[j], tm)
        else:
            r, lw, k, v, kk, bq, g = _rwkv_proj(
                x, mod[i], b_mu[j], b_w_in[j], b_w0[j], b_w1[j], b_w2[j], b_a0[j], b_a1[j], b_a2[j],
                b_g1[j], b_g2[j], b_k_k[j], b_k_a[j], tm)
            y = _wkv(r, lw, k, v, kk, bq, min(WKV_BLOCK, t), min(WKV_ROWS, bsz))
            x = _rwkv_out(x, mod[i], y, r, k, v, g, b_r_k[j], b_ln_g[j], b_ln_b[j], b_w_out[j], tm)
        x = _mlp_layer(x, mod[i], mlp_w1[i], mlp_w2[i], final_g, tm, final_norm=(i == depth - 1))
    return x
```

```python
import functools
import math

import jax
import jax.numpy as jnp
from jax import lax
from jax.experimental import pallas as pl
from jax.experimental.pallas import tpu as pltpu

RMS_EPS = 1e-6
LN_EPS = 1e-5
L2_EPS = 1e-12
N_MOD = 6

SGU_CHUNK = 128
SGU_GROUPS = 8

HEAD = 64
GN_EPS = HEAD * 1e-5
WKV_CHUNK = 64
GROUP_LANES = 256
GROUP_HEADS = GROUP_LANES // HEAD
WKV_ROWS = 4
WKV_BLOCK = 256
PROJ_SUBTILES = 4
TOKEN_TILE = 512
FF_BLOCKS = 4

VMEM_LIMIT_BYTES = 56 * 1024 * 1024

F32 = jnp.float32
BF16 = jnp.bfloat16


def _bf(x):
    return x.astype(BF16)


def _dot(a, b):
    return jnp.dot(a, b, preferred_element_type=F32)


def _dot_nt(a, b):
    return lax.dot_general(a, b, (((1,), (1,)), ((), ())), preferred_element_type=F32)


def _dot_tn(a, b):
    return lax.dot_general(a, b, (((0,), (0,)), ((), ())), preferred_element_type=F32)


def _split_bf16(x):
    hi = _bf(x)
    lo = _bf(x - hi.astype(F32))
    return hi, lo


def _sigmoid(x):
    return 1.0 / (1.0 + jnp.exp(-x))


def _rms_modulate(x, shift, scale):
    ms = jnp.mean(x * x, axis=-1, keepdims=True)
    return x * lax.rsqrt(ms + RMS_EPS) * (1.0 + scale) + shift


def _head_sum(x, ones_bd):
    xb = _bf(x)
    parts = []
    for g in range(x.shape[1] // GROUP_LANES):
        sl = slice(g * GROUP_LANES, (g + 1) * GROUP_LANES)
        parts.append(_dot(xb[:, sl], ones_bd))
    return jnp.concatenate(parts, axis=1)


def _head_block_mask(rows, cols):
    r = lax.broadcasted_iota(jnp.int32, (rows, cols), 0) // HEAD
    c = lax.broadcasted_iota(jnp.int32, (rows, cols), 1) // HEAD
    return r == c


def _compiler_params(semantics):
    return pltpu.CompilerParams(dimension_semantics=semantics, vmem_limit_bytes=VMEM_LIMIT_BYTES)


def _const_spec(shape):
    zeros = (0,) * len(shape)
    return pl.BlockSpec(shape, lambda *_: zeros)


def _weight_spec(shape):
    zeros = (0,) * len(shape)
    return pl.BlockSpec(shape, lambda *_: zeros, pipeline_mode=pl.Buffered(1))


def _token_spec(tm, d):
    return pl.BlockSpec((1, tm, d), lambda i, j: (i, j, 0))


def _mod_spec(d):
    return pl.BlockSpec((1, N_MOD, d), lambda i, j: (i, 0, 0))


def _ada_kernel(c_ref, w_ref, b_ref, o_ref):
    c = c_ref[...]
    cond = c * _sigmoid(c)
    c_hi, c_lo = _split_bf16(cond)
    w_hi, w_lo = _split_bf16(w_ref[0])
    o_ref[0] = _dot(c_hi, w_hi) + _dot(c_lo, w_hi) + _dot(c_hi, w_lo) + b_ref[0]


def _ada_mod(c, ada_w, ada_b):
    depth, d, nd = ada_w.shape
    b = c.shape[0]
    bn = nd // 4
    return pl.pallas_call(
        _ada_kernel,
        grid=(depth, nd // bn),
        in_specs=[
            pl.BlockSpec((b, d), lambda i, j: (0, 0)),
            pl.BlockSpec((1, d, bn), lambda i, j: (i, 0, j)),
            pl.BlockSpec((1, 1, bn), lambda i, j: (i, 0, j)),
        ],
        out_specs=pl.BlockSpec((1, b, bn), lambda i, j: (i, 0, j)),
        out_shape=jax.ShapeDtypeStruct((depth, b, nd), F32),
        compiler_params=_compiler_params(("parallel", "parallel")),
    )(c, ada_w, ada_b.reshape(depth, 1, nd))


def _mlp_residual(x, mod, w1_ref, w2_ref):
    h = _bf(_rms_modulate(x, mod[3:4], mod[4:5]))
    ff_block = w1_ref.shape[1] // FF_BLOCKS
    acc = jnp.zeros(x.shape, F32)
    for c in range(FF_BLOCKS):
        sl = slice(c * ff_block, (c + 1) * ff_block)
        hid = jnp.maximum(_dot(h, w1_ref[:, sl]), 0.0)
        acc = acc + _dot(_bf(hid * hid), w2_ref[sl, :])
    return x + mod[5:6] * acc


def _sgu_mlp_kernel(x_ref, mod_ref, w_in_ref, ln_g_ref, ln_b_ref, w_s_ref, b_s_ref, w_out_ref,
                    w1_ref, w2_ref, o_ref):
    x = x_ref[0]
    mod = mod_ref[0]
    tm = x.shape[0]
    width = w_out_ref.shape[0]
    gd = width // SGU_GROUPS
    h = _rms_modulate(x, mod[0:1], mod[1:2])
    uv = _dot(_bf(h), w_in_ref[...])
    uv = 0.5 * uv * (1.0 + lax.erf(uv * (1.0 / math.sqrt(2.0))))
    u = uv[:, :width]
    v = uv[:, width:]
    mu = jnp.mean(v, axis=-1, keepdims=True)
    vc = v - mu
    var = jnp.mean(vc * vc, axis=-1, keepdims=True)
    v = _bf(vc * lax.rsqrt(var + LN_EPS) * ln_g_ref[...] + ln_b_ref[...])
    row = lax.broadcasted_iota(jnp.int32, (SGU_CHUNK, SGU_CHUNK), 0)
    col = lax.broadcasted_iota(jnp.int32, (SGU_CHUNK, SGU_CHUNK), 1)
    causal = col <= row
    n_chunks = tm // SGU_CHUNK
    cols = []
    for g in range(SGU_GROUPS):
        w_g = _bf(jnp.where(causal, w_s_ref[g], 0.0))
        v_g = jnp.concatenate(
            [v[c * SGU_CHUNK:(c + 1) * SGU_CHUNK, g * gd:(g + 1) * gd] for c in range(n_chunks)], axis=1)
        sv_g = _dot(w_g, v_g) + b_s_ref[:, g:g + 1]
        cols.append(jnp.concatenate(
            [sv_g[:, c * gd:(c + 1) * gd] for c in range(n_chunks)], axis=0))
    sv = jnp.concatenate(cols, axis=1)
    x = x + mod[2:3] * _dot(_bf(u * sv), w_out_ref[...])
    o_ref[0] = _mlp_residual(x, mod, w1_ref, w2_ref)


def _sgu_mlp_layer(x, mod, w_in, ln_g, ln_b, w_s, b_s, w_out, w1, w2, tm):
    b, t, d = x.shape
    width = w_out.shape[0]
    return pl.pallas_call(
        _sgu_mlp_kernel,
        grid=(b, t // tm),
        in_specs=[
            _token_spec(tm, d),
            _mod_spec(d),
            _weight_spec((d, 2 * width)),
            _const_spec((1, width)),
            _const_spec((1, width)),
            _const_spec((SGU_GROUPS, SGU_CHUNK, SGU_CHUNK)),
            _const_spec((SGU_CHUNK, SGU_GROUPS)),
            _weight_spec((width, d)),
            _weight_spec(w1.shape),
            _weight_spec(w2.shape),
        ],
        out_specs=_token_spec(tm, d),
        out_shape=jax.ShapeDtypeStruct(x.shape, F32),
        compiler_params=_compiler_params(("parallel", "parallel")),
    )(x, mod, _bf(w_in), ln_g.reshape(1, width), ln_b.reshape(1, width), w_s, b_s.T, _bf(w_out),
      _bf(w1), _bf(w2))


def _rwkv_proj_kernel(x_ref, mod_ref, mu_ref, w_in_ref, w0_ref, w1_ref, w2_ref, a0_ref, a1_ref, a2_ref,
                      g1_ref, g2_ref, kk_ref, ka_ref,
                      r_out, lw_out, k_out, v_out, kk_out, b_out, g_out, carry_ref):
    @pl.when(pl.program_id(1) == 0)
    def _():
        carry_ref[...] = jnp.zeros_like(carry_ref)

    mod = mod_ref[0]
    tm, d = x_ref.shape[1], x_ref.shape[2]
    ts = tm // PROJ_SUBTILES
    rows = [slice(s * ts, (s + 1) * ts) for s in range(PROJ_SUBTILES)]
    mu = mu_ref[...]
    ones_bd = jnp.where(_head_block_mask(GROUP_LANES, GROUP_LANES), 1.0, 0.0).astype(BF16)
    first = lax.broadcasted_iota(jnp.int32, (ts, d), 0) == 0

    last = carry_ref[0:1, :]
    for rs in rows:
        h = _rms_modulate(x_ref[0, rs, :], mod[0:1], mod[1:2])
        prev = jnp.where(first, last, pltpu.roll(h, shift=1, axis=0))
        last = h[ts - 1:ts, :]
        xx = prev - h
        xr, xw, xk, xv, xa, xg = [_bf(h + xx * mu[i:i + 1]) for i in range(6)]
        w_lo = _dot(xw, w1_ref[...])
        a_lo = _dot(xa, a1_ref[...])
        g_lo = _dot(xg, g1_ref[...])
        r = _dot(xr, w_in_ref[:, 0:d])
        k = _dot(xk, w_in_ref[:, d:2 * d])
        v = _dot(xv, w_in_ref[:, 2 * d:3 * d])
        wl = _dot(_bf(jnp.tanh(w_lo)), w2_ref[...])
        al = _dot(_bf(a_lo), a2_ref[...])
        g = _dot(_bf(_sigmoid(g_lo)), g2_ref[...])
        kk = k * kk_ref[...]
        ssq = _head_sum(kk * kk, ones_bd)
        lw = -math.exp(-0.5) * _sigmoid(w0_ref[...] + wl)
        a = _sigmoid(a0_ref[...] + al)
        kk_n = kk * lax.rsqrt(jnp.maximum(ssq, L2_EPS * L2_EPS))
        r_out[0, rs, :] = _bf(r)
        lw_out[0, rs, :] = lw
        k_out[0, rs, :] = _bf(k * (1.0 + (a - 1.0) * ka_ref[...]))
        v_out[0, rs, :] = _bf(v)
        kk_out[0, rs, :] = _bf(kk_n)
        b_out[0, rs, :] = _bf(kk_n * a)
        g_out[0, rs, :] = _bf(g)
    carry_ref[0:1, :] = last


def _rwkv_proj(x, mod, mu, w_in, w0, w1, w2, a0, a1, a2, g1, g2, k_k, k_a, tm):
    b, t, d = x.shape
    tok = _token_spec(tm, d)
    row = lambda p: p.reshape(1, d)
    bf_out = jax.ShapeDtypeStruct(x.shape, BF16)
    f32_out = jax.ShapeDtypeStruct(x.shape, F32)
    return pl.pallas_call(
        _rwkv_proj_kernel,
        grid=(b, t // tm),
        in_specs=[
            tok,
            _mod_spec(d),
            _const_spec(mu.shape),
            _weight_spec(w_in.shape),
            _const_spec((1, d)), _const_spec(w1.shape), _const_spec(w2.shape),
            _const_spec((1, d)), _const_spec(a1.shape), _const_spec(a2.shape),
            _const_spec(g1.shape), _const_spec(g2.shape),
            _const_spec((1, d)), _const_spec((1, d)),
        ],
        out_specs=[tok] * 7,
        out_shape=[bf_out, f32_out, bf_out, bf_out, bf_out, bf_out, bf_out],
        scratch_shapes=[pltpu.VMEM((8, d), F32)],
        compiler_params=_compiler_params(("parallel", "arbitrary")),
    )(x, mod, mu, _bf(w_in), row(w0), _bf(w1), _bf(w2), row(a0), _bf(a1), _bf(a2), _bf(g1), _bf(g2),
      row(k_k), row(k_a))


def _block_diag(x, mask):
    tiled = jnp.concatenate([x] * GROUP_HEADS, axis=0)
    return jnp.where(mask, tiled, jnp.zeros_like(tiled))


def _wkv_kernel(r_ref, lw_ref, k_ref, v_ref, kk_ref, b_ref, y_ref, s_ref):
    @pl.when(pl.program_id(1) == 0)
    def _():
        s_ref[...] = jnp.zeros_like(s_ref)

    L = WKV_CHUNK
    nb, tb, d = lw_ref.shape
    n_groups = d // GROUP_LANES

    t_row = lax.broadcasted_iota(jnp.int32, (L, GROUP_LANES), 0)
    s_lane = lax.broadcasted_iota(jnp.int32, (L, GROUP_LANES), 1) % HEAD
    strict = s_lane < t_row
    incl = s_lane <= t_row
    eye = jnp.where(s_lane == t_row, 1.0, 0.0)
    bd_mask = _head_block_mask(GROUP_HEADS * L, GROUP_LANES)
    tri = jnp.where(lax.broadcasted_iota(jnp.int32, (L, L), 1) <= lax.broadcasted_iota(jnp.int32, (L, L), 0),
                    1.0, 0.0)

    def decayed_operands(n, sl):
        lw = lw_ref[n, sl, :]
        tri_b = _bf(tri)
        p0, p1 = _split_bf16(lw)
        cs = _dot(tri_b, p0) + _dot(tri_b, p1)
        cs_last = cs[L - 1:L, :]
        e_in = jnp.exp(cs)
        e_ex = jnp.exp(cs - lw)
        e_inv = jnp.exp(-cs)
        e_hat = jnp.exp(cs_last - cs)
        w_last = jnp.exp(cs_last)
        kk = kk_ref[n, sl, :].astype(F32)
        bb = b_ref[n, sl, :].astype(F32)
        kx = k_ref[n, sl, :].astype(F32)
        a_t = _bf(-kk * e_ex)
        r_t = _bf(r_ref[n, sl, :].astype(F32) * e_in)
        return (a_t, r_t, _bf(bb * e_inv), _bf(kx * e_inv), _bf(bb * e_hat), _bf(kx * e_hat),
                v_ref[n, sl, :], w_last)

    def chunk_body(c, carry):
        sl = pl.ds(pl.multiple_of(c * L, L), L)
        per_row = [decayed_operands(n, sl) for n in range(nb)]
        chains = [(n, g) for n in range(nb) for g in range(n_groups)]
        lanes = lambda g: slice(g * GROUP_LANES, (g + 1) * GROUP_LANES)
        pick = lambda idx: [per_row[n][idx][:, lanes(g)] for n, g in chains]
        a_t, r_t, b_t, k_t, b_h, k_h, vv, w_last = [pick(idx) for idx in range(8)]
        nc = len(chains)
        cat = lambda a, b: jnp.concatenate([a, b], axis=0)
        bd = lambda z: _block_diag(z, bd_mask)
        lhs = [cat(a, r) for a, r in zip(a_t, r_t)]
        aa = [_dot_nt(lhs[i], cat(bd(b_t[i]), bd(k_t[i]))) for i in range(nc)]
        a_ab = [jnp.where(strict, z[:L, :GROUP_LANES], 0.0) for z in aa]
        a_kv = [_bf(cat(jnp.where(strict, z[:L, GROUP_LANES:], 0.0),
                        jnp.where(incl, z[L:, GROUP_LANES:], 0.0))) for z in aa]
        a_rb = [_bf(jnp.where(incl, z[L:, :GROUP_LANES], 0.0)) for z in aa]

        inv = [eye + z for z in a_ab]
        pw = [_dot(_bf(z), bd(_bf(z))) for z in a_ab]
        n_sq = int(math.log2(L)) - 1
        for it in range(n_sq):
            if it + 1 < n_sq:
                res = [_dot(_bf(cat(i, p)), bd(_bf(p))) for i, p in zip(inv, pw)]
                inv = [i + z[:L] for i, z in zip(inv, res)]
                pw = [z[L:] for z in res]
            else:
                inv = [i + _dot(_bf(i), bd(_bf(p))) for i, p in zip(inv, pw)]
        inv = [_bf(i) for i in inv]

        av = [_dot(a_kv[i], bd(vv[i])) for i in range(nc)]
        s_old = [s_ref[i] for i in range(nc)]
        ss = [_dot_nt(lhs[i], _bf(s_old[i])) for i in range(nc)]
        u = [_dot(inv[i], bd(_bf(ss[i][:L] + av[i][:L]))) for i in range(nc)]
        u = [_bf(z) for z in u]
        y = [ss[i][L:] + _dot(a_rb[i], bd(u[i])) + av[i][L:] for i in range(nc)]
        upd = [_dot_tn(cat(u[i], vv[i]), cat(b_h[i], k_h[i])) for i in range(nc)]
        for i, (n, g) in enumerate(chains):
            s_ref[i] = s_old[i] * w_last[i] + jnp.where(bd_mask, upd[i], 0.0)
            y_ref[n, sl, lanes(g)] = y[i]
        return carry

    lax.fori_loop(0, tb // L, chunk_body, 0)


def _wkv(r, lw, k, v, kk, bq, tb, nb):
    b, t, d = lw.shape
    tok = pl.BlockSpec((nb, tb, d), lambda i, j: (i, j, 0))
    return pl.pallas_call(
        _wkv_kernel,
        grid=(b // nb, t // tb),
        in_specs=[tok] * 6,
        out_specs=tok,
        out_shape=jax.ShapeDtypeStruct(lw.shape, F32),
        scratch_shapes=[pltpu.VMEM((nb * (d // GROUP_LANES), GROUP_LANES, GROUP_LANES), F32)],
        compiler_params=_compiler_params(("parallel", "arbitrary")),
    )(r, lw, k, v, kk, bq)


def _rwkv_out_mlp_kernel(x_ref, mod_ref, y_ref, r_ref, k_ref, v_ref, g_ref, rk_ref, ln_g_ref, ln_b_ref,
                         w_out_ref, w1_ref, w2_ref, fg_ref, o_ref):
    x = x_ref[0]
    mod = mod_ref[0]
    y = y_ref[0]
    ones_bd = jnp.where(_head_block_mask(GROUP_LANES, GROUP_LANES), 1.0, 0.0).astype(BF16)
    m = _head_sum(y, ones_bd) * (1.0 / HEAD)
    yc = y - m
    var = _head_sum(yc * yc, ones_bd) * (1.0 / HEAD)
    yn = yc * lax.rsqrt(var + GN_EPS) * ln_g_ref[...] + ln_b_ref[...]
    r = r_ref[0].astype(F32)
    k = k_ref[0].astype(F32)
    v = v_ref[0].astype(F32)
    bonus = _head_sum(r * k * rk_ref[...], ones_bd) * v
    out = (yn + bonus) * g_ref[0].astype(F32)
    x = x + mod[2:3] * _dot(_bf(out), w_out_ref[...])
    x = _mlp_residual(x, mod, w1_ref, w2_ref)
    ms = jnp.mean(x * x, axis=-1, keepdims=True)
    o_ref[0] = x * lax.rsqrt(ms + RMS_EPS) * fg_ref[...]


def _rwkv_out_mlp_layer(x, mod, y, r, k, v, g, r_k, ln_g, ln_b, w_out, w1, w2, final_g, tm):
    b, t, d = x.shape
    tok = _token_spec(tm, d)
    row = lambda p: p.reshape(1, d)
    return pl.pallas_call(
        _rwkv_out_mlp_kernel,
        grid=(b, t // tm),
        in_specs=[
            tok, _mod_spec(d), tok, tok, tok, tok, tok,
            _const_spec((1, d)), _const_spec((1, d)), _const_spec((1, d)),
            _weight_spec((d, d)), _weight_spec(w1.shape), _weight_spec(w2.shape),
            _const_spec((1, d)),
        ],
        out_specs=tok,
        out_shape=jax.ShapeDtypeStruct(x.shape, F32),
        compiler_params=_compiler_params(("parallel", "parallel")),
    )(x, mod, y, r, k, v, g, row(r_k), row(ln_g), row(ln_b), _bf(w_out), _bf(w1), _bf(w2), row(final_g))


def kernel(x, c, ada_w, ada_b, mlp_w1, mlp_w2, a_w_in, a_ln_g, a_ln_b, a_w_s, a_b_s, a_w_out, b_mu, b_w_in,
           b_w0, b_w1, b_w2, b_a0, b_a1, b_a2, b_g1, b_g2, b_k_k, b_k_a, b_r_k, b_ln_g, b_ln_b, b_w_out,
           final_g):
    depth = ada_w.shape[0]
    bsz, t, d = x.shape
    assert depth == 2, "layer 0 is the SGU layer, layer 1 the RWKV-7 layer with the final norm"
    tm = min(TOKEN_TILE, t)
    mod = _ada_mod(c, ada_w, ada_b).reshape(depth, bsz, N_MOD, d)
    x = _sgu_mlp_layer(x, mod[0], a_w_in[0], a_ln_g[0], a_ln_b[0], a_w_s[0], a_b_s[0], a_w_out[0],
                       mlp_w1[0], mlp_w2[0], tm)
    r, lw, k, v, kk, bq, g = _rwkv_proj(
        x, mod[1], b_mu[0], b_w_in[0], b_w0[0], b_w1[0], b_w2[0], b_a0[0], b_a1[0], b_a2[0],
        b_g1[0], b_g2[0], b_k_k[0], b_k_a[0], tm)
    y = _wkv(r, lw, k, v, kk, bq, min(WKV_BLOCK, t), min(WKV_ROWS, bsz))
    return _rwkv_out_mlp_layer(x, mod[1], y, r, k, v, g, b_r_k[0], b_ln_g[0], b_ln_b[0], b_w_out[0],
                               mlp_w1[1], mlp_w2[1], final_g, tm)
```
